```python
import jax, jax.numpy as jnp
from jax import lax
import numpy as np

D_MODEL = 2048
BATCH = 8
SEQ = 2048
DEPTH = 1

CHUNK = 64
PLE_DIM = 256
DN_HEADS = 16
DN_HEAD_DIM = 128
DN_WIDTH = DN_HEADS * DN_HEAD_DIM
SHORT_CONV = 4
CONV_CH = D_MODEL
CONV_KERNEL = 31
N_BRANCHES = 2
EPS = 1e-6
IN_SIZES = (3 * DN_WIDTH,
            DN_WIDTH,
            DN_HEADS,
            DN_HEADS,
            2 * CONV_CH,
            CONV_CH,
            N_BRANCHES * D_MODEL)
IN_COLS = sum(IN_SIZES)

kernel_name = "hybrid_gdn_conformer_streaming_block"


def _split_points():
    return np.cumsum(np.array(IN_SIZES))[:-1].tolist()


def rmsnorm(x, g):
    xf = x.astype(jnp.float32)
    y = xf * lax.rsqrt(jnp.mean(xf * xf, axis=-1, keepdims=True) + EPS)
    return (y * g.astype(jnp.float32)).astype(x.dtype)


def layernorm(x, g, b):
    xf = x.astype(jnp.float32)
    mu = jnp.mean(xf, axis=-1, keepdims=True)
    var = jnp.mean(jnp.square(xf - mu), axis=-1, keepdims=True)
    y = (xf - mu) * lax.rsqrt(var + EPS)
    return (y * g.astype(jnp.float32) + b.astype(jnp.float32)).astype(x.dtype)


def l2norm(x):
    return x * lax.rsqrt(jnp.sum(x * x, axis=-1, keepdims=True) + EPS)


def causal_depthwise_conv(x, w):
    K = w.shape[0]
    xp = jnp.pad(x, ((0, 0), (K - 1, 0), (0, 0)))
    return lax.conv_general_dilated(
        xp, w[:, None, :].astype(x.dtype), window_strides=(1,), padding='VALID',
        dimension_numbers=('NWC', 'WIO', 'NWC'), feature_group_count=x.shape[-1])


def gated_delta_chunked(q, k, v, g, beta):
    B, S, H, Dk = q.shape
    nc = S // CHUNK

    def to_chunks(t):
        t = t.reshape((B, nc, CHUNK, H) + t.shape[3:])
        return jnp.moveaxis(t, 3, 1)

    q, k, v = to_chunks(q) * (Dk ** -0.5), to_chunks(k), to_chunks(v)
    g, beta = to_chunks(g), to_chunks(beta)
    gc = jnp.cumsum(g, axis=-1)
    idx = jnp.arange(CHUNK)
    causal = idx[:, None] >= idx[None, :]
    strict = idx[:, None] > idx[None, :]
    decay = jnp.exp(jnp.where(causal, gc[..., :, None] - gc[..., None, :], -jnp.inf))
    kb = k * beta[..., None]
    A = jnp.einsum('bhnid,bhnjd->bhnij', kb, k) * decay * strict
    eye = jnp.eye(CHUNK, dtype=jnp.float32)
    T = lax.linalg.triangular_solve(eye + A, jnp.broadcast_to(eye, A.shape),
                                    left_side=True, lower=True, unit_diagonal=True)
    u = jnp.einsum('bhnij,bhnjd->bhnid', T, v * beta[..., None])
    w = jnp.einsum('bhnij,bhnjd->bhnid', T, kb * jnp.exp(gc)[..., None])
    qk = jnp.einsum('bhnid,bhnjd->bhnij', q, k) * decay
    q_dec = q * jnp.exp(gc)[..., None]
    g_last = gc[..., -1]
    k_end = k * jnp.exp(g_last[..., None] - gc)[..., None]
    xs = tuple(jnp.moveaxis(t, 2, 0) for t in (u, w, qk, q_dec, k_end, g_last))

    def step(Sst, inp):
        u_n, w_n, qk_n, qd_n, ke_n, gl_n = inp
        v_new = u_n - jnp.einsum('bhid,bhde->bhie', w_n, Sst)
        o = jnp.einsum('bhid,bhde->bhie', qd_n, Sst) + jnp.einsum('bhij,bhje->bhie', qk_n, v_new)
        Sst = Sst * jnp.exp(gl_n)[..., None, None] + jnp.einsum('bhid,bhie->bhde', ke_n, v_new)
        return Sst, o

    S0 = jnp.zeros((B, H, Dk, v.shape[-1]), jnp.float32)
    _, o = lax.scan(step, S0, xs)
    return jnp.transpose(o, (1, 0, 3, 2, 4)).reshape(B, S, H, v.shape[-1])


def setup_inputs(seed: int = 0) -> dict:
    key = jax.random.key(seed)
    ks = jax.random.split(key, 24)
    f32 = jnp.float32
    nrm = lambda k, shape, s: jax.random.normal(k, shape, f32) * s
    gain = lambda k, shape: 1.0 + 0.05 * jax.random.normal(k, shape, f32)
    dt = jnp.exp(jax.random.uniform(ks[7], (DEPTH, DN_HEADS), f32, np.log(1e-3), np.log(1e-1)))
    return {
        "x": nrm(ks[0], (BATCH, SEQ, D_MODEL), 1.0),
        "p": nrm(ks[1], (DEPTH, BATCH, SEQ, PLE_DIM), 1.0),
        "g_pre": gain(ks[2], (DEPTH, D_MODEL)),
        "w_in": nrm(ks[3], (DEPTH, D_MODEL, IN_COLS), D_MODEL ** -0.5),
        "b_gate": nrm(ks[4], (DEPTH, N_BRANCHES * D_MODEL), 0.02),
        "w_conv_qkv": nrm(ks[5], (DEPTH, SHORT_CONV, 3 * DN_WIDTH), SHORT_CONV ** -0.5),
        "a_log": jnp.log(jax.random.uniform(ks[6], (DEPTH, DN_HEADS), f32, 1.0, 16.0)),
        "dt_bias": dt + jnp.log(-jnp.expm1(-dt)),
        "g_dn_out": gain(ks[8], (DEPTH, DN_HEAD_DIM)),
        "w_dw": nrm(ks[9], (DEPTH, CONV_KERNEL, CONV_CH), CONV_KERNEL ** -0.5),
        "b_dw": nrm(ks[10], (DEPTH, CONV_CH), 0.02),
        "ln_g": gain(ks[11], (DEPTH, CONV_CH)),
        "ln_b": nrm(ks[12], (DEPTH, CONV_CH), 0.02),
        "w_br_a": nrm(ks[13], (DEPTH, DN_WIDTH, D_MODEL), DN_WIDTH ** -0.5),
        "w_br_b": nrm(ks[14], (DEPTH, CONV_CH, D_MODEL), CONV_CH ** -0.5),
        "w_out": nrm(ks[15], (DEPTH, D_MODEL, D_MODEL), D_MODEL ** -0.5),
        "g_post": gain(ks[16], (DEPTH, D_MODEL)),
        "w_ple_gate": nrm(ks[17], (DEPTH, D_MODEL, D_MODEL), D_MODEL ** -0.5),
        "w_ple_proj": nrm(ks[18], (DEPTH, PLE_DIM, D_MODEL), PLE_DIM ** -0.5),
        "g_ple": gain(ks[19], (DEPTH, D_MODEL)),
    }


def reference(x, p, g_pre, w_in, b_gate, w_conv_qkv, a_log, dt_bias, g_dn_out, w_dw, b_dw,
              ln_g, ln_b, w_br_a, w_br_b, w_out, g_post, w_ple_gate, w_ple_proj, g_ple):
    B, S, _ = x.shape
    f32 = jnp.float32
    for i in range(DEPTH):
        h = rmsnorm(x, g_pre[i])
        proj = h @ w_in[i]
        qkv, z_a, b_raw, a_raw, glu_in, z_b, gate_logits = jnp.split(proj, _split_points(), axis=-1)

        qkv = jax.nn.silu(causal_depthwise_conv(qkv, w_conv_qkv[i]))
        q, k, v = jnp.split(qkv, 3, axis=-1)
        heads = lambda t: t.reshape(B, S, DN_HEADS, DN_HEAD_DIM).astype(f32)
        q, k, v = l2norm(heads(q)), l2norm(heads(k)), heads(v)
        beta = jax.nn.sigmoid(b_raw.astype(f32))
        g_log = -jnp.exp(a_log[i].astype(f32)) * jax.nn.softplus(a_raw.astype(f32) + dt_bias[i].astype(f32))
        o_a = gated_delta_chunked(q, k, v, g_log, beta)
        o_a = rmsnorm(o_a, g_dn_out[i]).reshape(B, S, DN_WIDTH).astype(x.dtype) * jax.nn.silu(z_a)
        y_a = o_a @ w_br_a[i]

        u = jax.nn.glu(glu_in, axis=-1)
        u = causal_depthwise_conv(u, w_dw[i]) + b_dw[i]
        u = jax.nn.silu(layernorm(u, ln_g[i], ln_b[i])) * jax.nn.silu(z_b)
        y_b = u @ w_br_b[i]

        gate_a, gate_b = jnp.split(jax.nn.sigmoid(gate_logits + b_gate[i]), N_BRANCHES, axis=-1)
        mixed = (gate_a * y_a + gate_b * y_b) @ w_out[i]
        x = x + rmsnorm(mixed, g_post[i])

        e = p[i] @ w_ple_proj[i]
        x = x + rmsnorm(jax.nn.sigmoid(x @ w_ple_gate[i]) * e, g_ple[i])
    return x
```

```python
import functools

import jax
import jax.numpy as jnp
from jax import lax
from jax.experimental import pallas as pl
from jax.experimental.pallas import tpu as pltpu

EPS = 1e-6
CHUNK = 64
LANES = 128
HALO = 32
VMEM_LIMIT = 56 * 1024 * 1024

F32 = jnp.float32
BF16 = jnp.bfloat16
HIGHEST = lax.Precision.HIGHEST


def _params(*sem):
    return pltpu.CompilerParams(dimension_semantics=sem, vmem_limit_bytes=VMEM_LIMIT)


def _tile(n, t):
    t = min(t, n)
    while n % t:
        t //= 2
    return t


def _sigmoid(x):
    return 1.0 / (1.0 + jnp.exp(-x))


def _silu(x):
    return x * _sigmoid(x)


def _dot(a, b):
    return jnp.dot(a.astype(BF16), b.astype(BF16), preferred_element_type=F32)


def _dot_nt(a, b):
    return lax.dot_general(a.astype(BF16), b.astype(BF16), (((1,), (1,)), ((), ())),
                           preferred_element_type=F32)


def _dot_tn(a, b):
    return lax.dot_general(a.astype(BF16), b.astype(BF16), (((0,), (0,)), ((), ())),
                           preferred_element_type=F32)


def _dot_f32(a, b):
    return jnp.dot(a, b, preferred_element_type=F32, precision=HIGHEST)


def _rmsnorm_kernel(x_ref, g_ref, o_ref):
    x = x_ref[...]
    ms = jnp.mean(x * x, axis=-1, keepdims=True)
    o_ref[...] = (x * lax.rsqrt(ms + EPS) * g_ref[...]).astype(o_ref.dtype)


def _rmsnorm(x2, g_row, tr=512):
    T, D = x2.shape
    tr = _tile(T, tr)
    return pl.pallas_call(
        _rmsnorm_kernel,
        grid=(T // tr,),
        in_specs=[pl.BlockSpec((tr, D), lambda i: (i, 0)),
                  pl.BlockSpec((1, D), lambda i: (0, 0))],
        out_specs=pl.BlockSpec((tr, D), lambda i: (i, 0)),
        out_shape=jax.ShapeDtypeStruct((T, D), BF16),
        compiler_params=_params("parallel"),
        name="pre_rmsnorm",
    )(x2, g_row)


def _proj_kernel(h_ref, w_ref, o_ref, *, act):
    acc = jnp.dot(h_ref[...], w_ref[...], preferred_element_type=F32)
    if act == "silu":
        acc = _silu(acc)
    o_ref[...] = acc.astype(o_ref.dtype)


def _proj(h, w, act, out_dtype, tm=1024, tn=512, name="proj"):
    T, D = h.shape
    N = w.shape[1]
    tm, tn = _tile(T, tm), _tile(N, tn)
    return pl.pallas_call(
        functools.partial(_proj_kernel, act=act),
        grid=(T // tm, N // tn),
        in_specs=[pl.BlockSpec((tm, D), lambda i, j: (i, 0)),
                  pl.BlockSpec((D, tn), lambda i, j: (0, j))],
        out_specs=pl.BlockSpec((tm, tn), lambda i, j: (i, j)),
        out_shape=jax.ShapeDtypeStruct((T, N), out_dtype),
        compiler_params=_params("parallel", "arbitrary"),
        name=name,
    )(h, w)


def _glu_kernel(h_ref, wa_ref, wb_ref, o_ref):
    h = h_ref[...]
    a = jnp.dot(h, wa_ref[...], preferred_element_type=F32)
    b = jnp.dot(h, wb_ref[...], preferred_element_type=F32)
    o_ref[...] = (a * _sigmoid(b)).astype(o_ref.dtype)


def _proj_glu(h, w, out_dtype, tm=1024, tn=256):
    T, D = h.shape
    N = w.shape[1] // 2
    tm, tn = _tile(T, tm), _tile(N, tn)
    nb = N // tn
    return pl.pallas_call(
        _glu_kernel,
        grid=(T // tm, nb),
        in_specs=[pl.BlockSpec((tm, D), lambda i, j: (i, 0)),
                  pl.BlockSpec((D, tn), lambda i, j: (0, j)),
                  pl.BlockSpec((D, tn), lambda i, j: (0, j + nb))],
        out_specs=pl.BlockSpec((tm, tn), lambda i, j: (i, j)),
        out_shape=jax.ShapeDtypeStruct((T, N), out_dtype),
        compiler_params=_params("parallel", "arbitrary"),
        name="proj_glu",
    )(h, w, w)


def _gate_kernel(h_ref, w_ref, b_ref, o_ref):
    acc = jnp.dot(h_ref[...], w_ref[...], preferred_element_type=F32)
    o_ref[...] = _sigmoid(acc + b_ref[...]).astype(o_ref.dtype)


def _proj_gate(h, w, b_row, out_dtype, tm=1024, tn=512):
    T, D = h.shape
    N = w.shape[1]
    tm, tn = _tile(T, tm), _tile(N // 2, tn)
    return pl.pallas_call(
        _gate_kernel,
        grid=(T // tm, N // tn),
        in_specs=[pl.BlockSpec((tm, D), lambda i, j: (i, 0)),
                  pl.BlockSpec((D, tn), lambda i, j: (0, j)),
                  pl.BlockSpec((1, tn), lambda i, j: (0, j))],
        out_specs=pl.BlockSpec((tm, tn), lambda i, j: (i, j)),
        out_shape=jax.ShapeDtypeStruct((T, N), out_dtype),
        compiler_params=_params("parallel", "arbitrary"),
        name="proj_gate",
    )(h, w, b_row)


def _gdn_kernel(q_ref, k_ref, v_ref, wq_ref, wk_ref, wv_ref, ba_ref, alog_ref, dtb_ref,
                za_ref, gdn_ref, o_ref, xp, qs, ks, vs, bs, gs, *, n_heads, conv_k):
    head = pl.program_id(1)
    S, dk = qs.shape
    C = CHUNK
    pad = 8
    rb = min(256, S)

    def conv_silu(x_ref, w_ref, dst, l2):
        xp[pl.ds(0, pad), :] = jnp.zeros((pad, dk), F32)
        xp[pl.ds(pad, S), :] = x_ref[...]
        w = w_ref[...]
        for r0 in range(0, S, rb):
            acc = jnp.zeros((rb, dk), F32)
            for j in range(conv_k):
                sh = conv_k - 1 - j
                acc = acc + xp[pl.ds(pad + r0 - sh, rb), :] * w[j:j + 1, :]
            y = _silu(acc)
            if l2:
                y = y * lax.rsqrt(jnp.sum(y * y, axis=-1, keepdims=True) + EPS)
            dst[pl.ds(r0, rb), :] = y

    conv_silu(q_ref, wq_ref, qs, True)
    conv_silu(k_ref, wk_ref, ks, True)
    conv_silu(v_ref, wv_ref, vs, False)

    lane = lax.broadcasted_iota(jnp.int32, (rb, LANES), 1)
    for r0 in range(0, S, rb):
        raw = ba_ref[pl.ds(r0, rb), :]
        beta_all = _sigmoid(raw)
        z = raw + dtb_ref[...]
        softplus = jnp.maximum(z, 0.0) + jnp.log(1.0 + jnp.exp(-jnp.abs(z)))
        g_all = -jnp.exp(alog_ref[...]) * softplus
        beta = jnp.sum(jnp.where(lane == head, beta_all, 0.0), axis=-1, keepdims=True)
        g = jnp.sum(jnp.where(lane == head + n_heads, g_all, 0.0), axis=-1, keepdims=True)
        bs[pl.ds(r0, rb), :] = jnp.broadcast_to(beta, (rb, dk))
        gs[pl.ds(r0, rb), :] = jnp.broadcast_to(g, (rb, dk))

    ri = lax.broadcasted_iota(jnp.int32, (C, C), 0)
    ci = lax.broadcasted_iota(jnp.int32, (C, C), 1)
    causal = ri >= ci
    strict = ri > ci
    eye = (ri == ci).astype(F32)
    tril = causal.astype(F32)
    ones = jnp.ones((C, C), F32)
    scale = dk ** -0.5
    gdn = gdn_ref[...]

    def chunk(c, state):
        r = pl.multiple_of(c * C, C)
        qc = qs[pl.ds(r, C), :] * scale
        kc = ks[pl.ds(r, C), :]
        vc = vs[pl.ds(r, C), :]
        bb = bs[pl.ds(r, C), :]
        gc = _dot_f32(tril, gs[pl.ds(r, C), :])
        g_col = gc[:, :C]
        g_row = _dot_f32(ones, g_col * eye)
        diff = g_col - g_row
        decay = jnp.where(causal, jnp.exp(jnp.where(causal, diff, 0.0)), 0.0)
        kb = kc * bb
        a = _dot_nt(kb, kc) * jnp.where(strict, decay, 0.0)
        m = -a
        t = eye + m
        p = 2
        while p < C:
            m = _dot_f32(m, m)
            t = t + _dot_f32(t, m)
            p *= 2
        eg = jnp.exp(gc)
        u = _dot(t, vc * bb)
        w = _dot(t, kb * eg)
        qk = _dot_nt(qc, kc) * decay
        g_last = gc[C - 1:C, :]
        k_end = kc * jnp.exp(g_last - gc)
        v_new = u - _dot(w, state)
        o = _dot(qc * eg, state) + _dot(qk, v_new)
        state = state * jnp.exp(g_last) + _dot_tn(k_end, v_new)
        o = o * lax.rsqrt(jnp.mean(o * o, axis=-1, keepdims=True) + EPS) * gdn
        o_ref[pl.ds(r, C), :] = (o * za_ref[pl.ds(r, C), :].astype(F32)).astype(o_ref.dtype)
        return state

    lax.fori_loop(0, S // C, chunk, jnp.zeros((dk, dk), F32))


def _gdn(qkv3, wconv, ba3, alog_row, dtb_row, za3, gdn_row, n_heads):
    B, S, W3 = qkv3.shape
    dk = W3 // (3 * n_heads)
    conv_k = wconv.shape[0]
    seq = lambda off: pl.BlockSpec((None, S, dk), lambda b, h: (b, 0, h + off))
    wsp = lambda off: pl.BlockSpec((conv_k, dk), lambda b, h: (0, h + off))
    row = pl.BlockSpec((1, LANES), lambda b, h: (0, 0))
    return pl.pallas_call(
        functools.partial(_gdn_kernel, n_heads=n_heads, conv_k=conv_k),
        grid=(B, n_heads),
        in_specs=[seq(0), seq(n_heads), seq(2 * n_heads),
                  wsp(0), wsp(n_heads), wsp(2 * n_heads),
                  pl.BlockSpec((None, S, LANES), lambda b, h: (b, 0, 0)),
                  row, row,
                  pl.BlockSpec((None, S, dk), lambda b, h: (b, 0, h)),
                  pl.BlockSpec((1, dk), lambda b, h: (0, 0))],
        out_specs=pl.BlockSpec((None, S, dk), lambda b, h: (b, 0, h)),
        out_shape=jax.ShapeDtypeStruct((B, S, n_heads * dk), BF16),
        scratch_shapes=[pltpu.VMEM((S + 8, dk), F32)] + [pltpu.VMEM((S, dk), F32)] * 5,
        compiler_params=_params("parallel", "arbitrary"),
        name="gated_deltanet",
    )(qkv3, qkv3, qkv3, wconv, wconv, wconv, ba3, alog_row, dtb_row, za3, gdn_row)


def _conf_kernel(u_ref, halo_ref, zb_ref, w_ref, bdw_ref, lng_ref, lnb_ref, o_ref, win, cv, *, conv_k):
    i = pl.program_id(1)
    ts, Cn = cv.shape
    rs = 64
    off = HALO - (conv_k - 1)

    @pl.when(i == 0)
    def _():
        win[pl.ds(0, HALO), :] = jnp.zeros((HALO, Cn), F32)

    @pl.when(i > 0)
    def _():
        win[pl.ds(0, HALO), :] = halo_ref[...]

    win[pl.ds(HALO, ts), :] = u_ref[...]

    def col_block(cb, carry):
        c0 = pl.multiple_of(cb * LANES, LANES)
        w = w_ref[:, pl.ds(c0, LANES)]
        bias = bdw_ref[:, pl.ds(c0, LANES)]
        for r0 in range(0, ts, rs):
            acc = jnp.zeros((rs, LANES), F32)
            for k in range(conv_k):
                acc = acc + win[pl.ds(r0 + off + k, rs), pl.ds(c0, LANES)] * w[k:k + 1, :]
            cv[pl.ds(r0, rs), pl.ds(c0, LANES)] = acc + bias
        return carry

    lax.fori_loop(0, Cn // LANES, col_block, 0)

    y = cv[...]
    mu = jnp.mean(y, axis=-1, keepdims=True)
    d = y - mu
    var = jnp.mean(d * d, axis=-1, keepdims=True)
    y = d * lax.rsqrt(var + EPS) * lng_ref[...] + lnb_ref[...]
    o_ref[...] = (_silu(y) * zb_ref[...].astype(F32)).astype(o_ref.dtype)


def _conformer(u3, zb3, w_dw, bdw_row, lng_row, lnb_row, ts=256):
    B, S, Cn = u3.shape
    conv_k = w_dw.shape[0]
    ts = _tile(S, ts)
    hb = ts // HALO
    full = lambda shape: pl.BlockSpec(shape, lambda b, i: (0, 0))
    return pl.pallas_call(
        functools.partial(_conf_kernel, conv_k=conv_k),
        grid=(B, S // ts),
        in_specs=[pl.BlockSpec((None, ts, Cn), lambda b, i: (b, i, 0)),
                  pl.BlockSpec((None, HALO, Cn), lambda b, i: (b, jnp.maximum(i * hb - 1, 0), 0)),
                  pl.BlockSpec((None, ts, Cn), lambda b, i: (b, i, 0)),
                  full((conv_k, Cn)), full((1, Cn)), full((1, Cn)), full((1, Cn))],
        out_specs=pl.BlockSpec((None, ts, Cn), lambda b, i: (b, i, 0)),
        out_shape=jax.ShapeDtypeStruct((B, S, Cn), BF16),
        scratch_shapes=[pltpu.VMEM((HALO + ts, Cn), F32), pltpu.VMEM((ts, Cn), F32)],
        compiler_params=_params("parallel", "arbitrary"),
        name="conformer_conv",
    )(u3, u3, zb3, w_dw, bdw_row, lng_row, lnb_row)


def _merge_kernel(oa_ref, vb_ref, wa_ref, wb_ref, ga_ref, gb_ref, o_ref):
    ya = jnp.dot(oa_ref[...], wa_ref[...], preferred_element_type=F32)
    yb = jnp.dot(vb_ref[...], wb_ref[...], preferred_element_type=F32)
    mixed = ga_ref[...].astype(F32) * ya + gb_ref[...].astype(F32) * yb
    o_ref[...] = mixed.astype(o_ref.dtype)


def _merge(oa, vb, wa, wb, gates, tm=1024, tn=512):
    T, D = oa.shape
    N = wa.shape[1]
    tm, tn = _tile(T, tm), _tile(N, tn)
    nb = N // tn
    return pl.pallas_call(
        _merge_kernel,
        grid=(T // tm, nb),
        in_specs=[pl.BlockSpec((tm, D), lambda i, j: (i, 0)),
                  pl.BlockSpec((tm, D), lambda i, j: (i, 0)),
                  pl.BlockSpec((D, tn), lambda i, j: (0, j)),
                  pl.BlockSpec((D, tn), lambda i, j: (0, j)),
                  pl.BlockSpec((tm, tn), lambda i, j: (i, j)),
                  pl.BlockSpec((tm, tn), lambda i, j: (i, j + nb))],
        out_specs=pl.BlockSpec((tm, tn), lambda i, j: (i, j)),
        out_shape=jax.ShapeDtypeStruct((T, N), BF16),
        compiler_params=_params("parallel", "arbitrary"),
        name="branch_merge",
    )(oa, vb, wa, wb, gates, gates)


def _out_kernel(m_ref, w_ref, x_ref, g_ref, o_ref):
    y = jnp.dot(m_ref[...], w_ref[...], preferred_element_type=F32)
    y = y * lax.rsqrt(jnp.mean(y * y, axis=-1, keepdims=True) + EPS) * g_ref[...]
    o_ref[...] = x_ref[...] + y


def _out_proj(mixed, w, x2, g_row, tm=256):
    T, D = x2.shape
    tm = _tile(T, tm)
    return pl.pallas_call(
        _out_kernel,
        grid=(T // tm,),
        in_specs=[pl.BlockSpec((tm, D), lambda i: (i, 0)),
                  pl.BlockSpec((D, D), lambda i: (0, 0)),
                  pl.BlockSpec((tm, D), lambda i: (i, 0)),
                  pl.BlockSpec((1, D), lambda i: (0, 0))],
        out_specs=pl.BlockSpec((tm, D), lambda i: (i, 0)),
        out_shape=jax.ShapeDtypeStruct((T, D), F32),
        compiler_params=_params("parallel"),
        name="out_proj",
    )(mixed, w, x2, g_row)


def _ple_kernel(x_ref, p_ref, wg_ref, wp_ref, g_ref, o_ref):
    x1 = x_ref[...]
    gate = _sigmoid(jnp.dot(x1.astype(BF16), wg_ref[...], preferred_element_type=F32))
    e = jnp.dot(p_ref[...].astype(BF16), wp_ref[...], preferred_element_type=F32)
    y = gate * e
    y = y * lax.rsqrt(jnp.mean(y * y, axis=-1, keepdims=True) + EPS) * g_ref[...]
    o_ref[...] = x1 + y


def _ple(x1, p2, wg, wp, g_row, tm=256):
    T, D = x1.shape
    P = p2.shape[1]
    tm = _tile(T, tm)
    return pl.pallas_call(
        _ple_kernel,
        grid=(T // tm,),
        in_specs=[pl.BlockSpec((tm, D), lambda i: (i, 0)),
                  pl.BlockSpec((tm, P), lambda i: (i, 0)),
                  pl.BlockSpec((D, D), lambda i: (0, 0)),
                  pl.BlockSpec((P, D), lambda i: (0, 0)),
                  pl.BlockSpec((1, D), lambda i: (0, 0))],
        out_specs=pl.BlockSpec((tm, D), lambda i: (i, 0)),
        out_shape=jax.ShapeDtypeStruct((T, D), F32),
        compiler_params=_params("parallel"),
        name="ple",
    )(x1, p2, wg, wp, g_row)


def _layer(x, p, g_pre, w_in, b_gate, w_conv_qkv, a_log, dt_bias, g_dn_out, w_dw, b_dw,
           ln_g, ln_b, w_br_a, w_br_b, w_out, g_post, w_ple_gate, w_ple_proj, g_ple):
    B, S, D = x.shape
    T = B * S
    H = a_log.shape[0]
    dk = g_dn_out.shape[0]
    dn = H * dk
    cc = w_dw.shape[1]
    c0 = 3 * dn
    c1 = c0 + dn
    c2 = c1 + 2 * H
    c3 = c2 + 2 * cc
    c4 = c3 + cc
    row = lambda v: v.reshape(1, -1).astype(F32)
    lane_row = lambda v: jnp.zeros((1, LANES), F32).at[0, H:2 * H].set(v.astype(F32))

    x2 = x.reshape(T, D)
    h = _rmsnorm(x2, row(g_pre))

    wb = w_in.astype(BF16)
    w_ba = jnp.zeros((D, LANES), BF16).at[:, :2 * H].set(wb[:, c1:c2])
    qkv = _proj(h, wb[:, :c0], None, F32, name="proj_qkv")
    za = _proj(h, wb[:, c0:c1], "silu", BF16, name="proj_za")
    ba = _proj(h, w_ba, None, F32, name="proj_ba")
    u = _proj_glu(h, wb[:, c2:c3], F32)
    zb = _proj(h, wb[:, c3:c4], "silu", BF16, name="proj_zb")
    gates = _proj_gate(h, wb[:, c4:], row(b_gate), BF16)

    oa = _gdn(qkv.reshape(B, S, c0), w_conv_qkv.astype(F32), ba.reshape(B, S, LANES),
              lane_row(a_log), lane_row(dt_bias), za.reshape(B, S, dn), row(g_dn_out), H)
    vb = _conformer(u.reshape(B, S, cc), zb.reshape(B, S, cc), w_dw.astype(F32),
                    row(b_dw), row(ln_g), row(ln_b))

    mixed = _merge(oa.reshape(T, dn), vb.reshape(T, cc), w_br_a.astype(BF16),
                   w_br_b.astype(BF16), gates)
    x1 = _out_proj(mixed, w_out.astype(BF16), x2, row(g_post))
    out = _ple(x1, p.reshape(T, -1), w_ple_gate.astype(BF16), w_ple_proj.astype(BF16), row(g_ple))
    return out.reshape(B, S, D)


def kernel(x, p, g_pre, w_in, b_gate, w_conv_qkv, a_log, dt_bias, g_dn_out, w_dw, b_dw,
           ln_g, ln_b, w_br_a, w_br_b, w_out, g_post, w_ple_gate, w_ple_proj, g_ple):
    for i in range(p.shape[0]):
        x = _layer(x, p[i], g_pre[i], w_in[i], b_gate[i], w_conv_qkv[i], a_log[i], dt_bias[i],
                   g_dn_out[i], w_dw[i], b_dw[i], ln_g[i], ln_b[i], w_br_a[i], w_br_b[i],
                   w_out[i], g_post[i], w_ple_gate[i], w_ple_proj[i], g_ple[i])
    return x
```

```python
import functools

import jax
import jax.numpy as jnp
from jax import lax
from jax.experimental import pallas as pl
from jax.experimental.pallas import tpu as pltpu

EPS = 1e-6
GDN_CHUNK = 128
LANES = 128
HALO = 32
VMEM_LIMIT = 56 * 1024 * 1024

F32 = jnp.float32
BF16 = jnp.bfloat16
HIGHEST = lax.Precision.HIGHEST


def _params(*sem):
    return pltpu.CompilerParams(dimension_semantics=sem, vmem_limit_bytes=VMEM_LIMIT)


def _tile(n, t):
    t = min(t, n)
    while n % t:
        t //= 2
    return t


def _sigmoid(x):
    return 0.5 * jnp.tanh(0.5 * x) + 0.5


def _silu(x):
    hx = 0.5 * x
    return hx + hx * jnp.tanh(hx)


def _dot(a, b):
    return jnp.dot(a.astype(BF16), b.astype(BF16), preferred_element_type=F32)


def _dot_nt(a, b):
    return lax.dot_general(a.astype(BF16), b.astype(BF16), (((1,), (1,)), ((), ())),
                           preferred_element_type=F32)


def _dot_tn(a, b):
    return lax.dot_general(a.astype(BF16), b.astype(BF16), (((0,), (0,)), ((), ())),
                           preferred_element_type=F32)


def _dot_f32(a, b):
    return jnp.dot(a, b, preferred_element_type=F32, precision=HIGHEST)


def _rmsnorm_kernel(x_ref, g_ref, o_ref):
    x = x_ref[...]
    ms = jnp.mean(x * x, axis=-1, keepdims=True)
    o_ref[...] = (x * lax.rsqrt(ms + EPS) * g_ref[...]).astype(o_ref.dtype)


def _rmsnorm(x2, g_row, tr=512):
    T, D = x2.shape
    tr = _tile(T, tr)
    return pl.pallas_call(
        _rmsnorm_kernel,
        grid=(T // tr,),
        in_specs=[pl.BlockSpec((tr, D), lambda i: (i, 0)),
                  pl.BlockSpec((1, D), lambda i: (0, 0))],
        out_specs=pl.BlockSpec((tr, D), lambda i: (i, 0)),
        out_shape=jax.ShapeDtypeStruct((T, D), BF16),
        compiler_params=_params("parallel"),
        name="pre_rmsnorm",
    )(x2, g_row)


def _proj_kernel(h_ref, w_ref, o_ref, *, act):
    acc = jnp.dot(h_ref[...], w_ref[...], preferred_element_type=F32)
    if act == "silu":
        acc = _silu(acc)
    o_ref[...] = acc.astype(o_ref.dtype)


def _proj(h, w, act, out_dtype, tm=1024, tn=512, name="proj"):
    T, D = h.shape
    N = w.shape[1]
    tm, tn = _tile(T, tm), _tile(N, tn)
    return pl.pallas_call(
        functools.partial(_proj_kernel, act=act),
        grid=(T // tm, N // tn),
        in_specs=[pl.BlockSpec((tm, D), lambda i, j: (i, 0)),
                  pl.BlockSpec((D, tn), lambda i, j: (0, j))],
        out_specs=pl.BlockSpec((tm, tn), lambda i, j: (i, j)),
        out_shape=jax.ShapeDtypeStruct((T, N), out_dtype),
        compiler_params=_params("parallel", "arbitrary"),
        name=name,
    )(h, w)


def _glu_kernel(h_ref, wa_ref, wb_ref, o_ref):
    h = h_ref[...]
    a = jnp.dot(h, wa_ref[...], preferred_element_type=F32)
    b = jnp.dot(h, wb_ref[...], preferred_element_type=F32)
    o_ref[...] = (a * _sigmoid(b)).astype(o_ref.dtype)


def _proj_glu(h, w, out_dtype, tm=1024, tn=256):
    T, D = h.shape
    N = w.shape[1] // 2
    tm, tn = _tile(T, tm), _tile(N, tn)
    nb = N // tn
    return pl.pallas_call(
        _glu_kernel,
        grid=(T // tm, nb),
        in_specs=[pl.BlockSpec((tm, D), lambda i, j: (i, 0)),
                  pl.BlockSpec((D, tn), lambda i, j: (0, j)),
                  pl.BlockSpec((D, tn), lambda i, j: (0, j + nb))],
        out_specs=pl.BlockSpec((tm, tn), lambda i, j: (i, j)),
        out_shape=jax.ShapeDtypeStruct((T, N), out_dtype),
        compiler_params=_params("parallel", "arbitrary"),
        name="proj_glu",
    )(h, w, w)


def _gate_kernel(h_ref, w_ref, b_ref, o_ref):
    acc = jnp.dot(h_ref[...], w_ref[...], preferred_element_type=F32)
    o_ref[...] = _sigmoid(acc + b_ref[...]).astype(o_ref.dtype)


def _proj_gate(h, w, b_row, out_dtype, tm=1024, tn=512):
    T, D = h.shape
    N = w.shape[1]
    tm, tn = _tile(T, tm), _tile(N // 2, tn)
    return pl.pallas_call(
        _gate_kernel,
        grid=(T // tm, N // tn),
        in_specs=[pl.BlockSpec((tm, D), lambda i, j: (i, 0)),
                  pl.BlockSpec((D, tn), lambda i, j: (0, j)),
                  pl.BlockSpec((1, tn), lambda i, j: (0, j))],
        out_specs=pl.BlockSpec((tm, tn), lambda i, j: (i, j)),
        out_shape=jax.ShapeDtypeStruct((T, N), out_dtype),
        compiler_params=_params("parallel", "arbitrary"),
        name="proj_gate",
    )(h, w, b_row)


def _gdn_prep_kernel(ba_ref, alog_ref, dtb_ref, col_ref, row_ref, *, n_heads):
    S = ba_ref.shape[0]
    C = GDN_CHUNK
    lane = lax.broadcasted_iota(jnp.int32, (C, LANES), 1)
    ri = lax.broadcasted_iota(jnp.int32, (C, C), 0)
    ci = lax.broadcasted_iota(jnp.int32, (C, C), 1)
    tril = (ri >= ci).astype(F32)
    is_decay = (lane >= n_heads) & (lane < 2 * n_heads)
    for c in range(S // C):
        raw = ba_ref[pl.ds(c * C, C), :]
        z = raw + dtb_ref[...]
        softplus = jnp.maximum(z, 0.0) + jnp.log(1.0 + jnp.exp(-jnp.abs(z)))
        g = jnp.where(is_decay, -jnp.exp(alog_ref[...]) * softplus, 0.0)
        gc = _dot_f32(tril, g)
        both = jnp.where(lane < n_heads, _sigmoid(raw), gc)
        col_ref[pl.ds(c * C, C), :] = both
        row_ref[:, pl.ds(c * C, C)] = both.T


def _gdn_prep(ba3, alog_row, dtb_row, n_heads):
    B, S, _ = ba3.shape
    row = pl.BlockSpec((1, LANES), lambda b: (0, 0))
    return pl.pallas_call(
        functools.partial(_gdn_prep_kernel, n_heads=n_heads),
        grid=(B,),
        in_specs=[pl.BlockSpec((None, S, LANES), lambda b: (b, 0, 0)), row, row],
        out_specs=[pl.BlockSpec((None, S, LANES), lambda b: (b, 0, 0)),
                   pl.BlockSpec((None, LANES, S), lambda b: (b, 0, 0))],
        out_shape=[jax.ShapeDtypeStruct((B, S, LANES), F32),
                   jax.ShapeDtypeStruct((B, LANES, S), F32)],
        compiler_params=_params("parallel"),
        name="gdn_prep",
    )(ba3, alog_row, dtb_row)


def _gdn_kernel(q_ref, k_ref, v_ref, wq_ref, wk_ref, wv_ref, col_ref, rowg_ref,
                za_ref, gdn_ref, o_ref, xp, qs, ks, vs, *, n_heads, conv_k, group, unroll):
    S, W = qs.shape
    dk = W // group
    C = GDN_CHUNK
    h0 = pl.program_id(1) * group
    pad = 8
    rb = min(128, S)

    def conv_silu(x_ref, w_ref, dst, l2):
        xp[pl.ds(0, pad), :] = jnp.zeros((pad, dk), F32)
        for g in range(group):
            sl = pl.ds(g * dk, dk)
            xp[pl.ds(pad, S), :] = x_ref[:, sl]
            w = w_ref[:, sl]
            for r0 in range(0, S, rb):
                acc = jnp.zeros((rb, dk), F32)
                for j in range(conv_k):
                    sh = conv_k - 1 - j
                    acc = acc + xp[pl.ds(pad + r0 - sh, rb), :] * w[j:j + 1, :]
                y = _silu(acc)
                if l2:
                    y = y * lax.rsqrt(jnp.sum(y * y, axis=-1, keepdims=True) + EPS)
                dst[pl.ds(r0, rb), sl] = y

    conv_silu(q_ref, wq_ref, qs, True)
    conv_silu(k_ref, wk_ref, ks, True)
    conv_silu(v_ref, wv_ref, vs, False)

    lane = lax.broadcasted_iota(jnp.int32, (C, LANES), 1)
    sub = lax.broadcasted_iota(jnp.int32, (8, C), 0)
    ri = lax.broadcasted_iota(jnp.int32, (C, C), 0)
    ci = lax.broadcasted_iota(jnp.int32, (C, C), 1)
    causal = ri >= ci
    strict = ri > ci
    scale = dk ** -0.5
    gdn = gdn_ref[...]

    def lane_column(blk, idx):
        col = jnp.sum(jnp.where(lane == idx, blk, 0.0), axis=-1, keepdims=True)
        return jnp.broadcast_to(col, (C, dk))

    def load_chunk(g, r):
        sl = pl.ds(g * dk, dk)
        qc = qs[pl.ds(r, C), sl] * scale
        kc = ks[pl.ds(r, C), sl]
        vc = vs[pl.ds(r, C), sl]
        blk = col_ref[pl.ds(r, C), :]
        b_col = lane_column(blk, h0 + g)
        g_col = lane_column(blk, h0 + g + n_heads)
        rows = rowg_ref[:, pl.ds(r, C)]
        g_row = jnp.sum(jnp.where(sub == (h0 + g) % 8, rows, 0.0), axis=0, keepdims=True)
        decay = jnp.where(causal, jnp.exp(jnp.where(causal, g_col - g_row, 0.0)), 0.0)
        kb = kc * b_col
        eg = jnp.exp(g_col)
        g_last = g_col[C - 1:C, :]
        return dict(qc=qc, kc=kc, kb=kb, decay=decay, qd=qc * eg,
                    rhs=jnp.concatenate([vc * b_col, kb * eg], axis=1),
                    k_end=kc * jnp.exp(g_last - g_col), e_last=jnp.exp(g_last))

    def body(i, states):
        states = list(states)
        chains = [(uu, g) for uu in range(unroll) for g in range(group)]
        rows = [pl.multiple_of((i * unroll + uu) * C, C) for uu in range(unroll)]
        ops = [load_chunk(g, rows[uu]) for uu, g in chains]
        kqs = [_dot_nt(jnp.concatenate([c["kb"], c["qc"]], axis=0), c["kc"]) for c in ops]
        qks = [kq[C:] * c["decay"] for kq, c in zip(kqs, ops)]
        ms = [-(kq[:C] * jnp.where(strict, c["decay"], 0.0)) for kq, c in zip(kqs, ops)]
        ts = list(ms)
        p = 2
        while p < C:
            ms = [_dot(m, m) for m in ms]
            ts = [t + m + _dot(t, m) for t, m in zip(ts, ms)]
            p *= 2
        uws = [c["rhs"] + _dot(t, c["rhs"]) for t, c in zip(ts, ops)]
        for (uu, g), c, uw, qk in zip(chains, ops, uws, qks):
            st = states[g]
            ws = _dot(jnp.concatenate([uw[:, dk:], c["qd"]], axis=0), st)
            v_new = uw[:, :dk] - ws[:C]
            o = ws[C:] + _dot(qk, v_new)
            states[g] = st * c["e_last"] + _dot_tn(c["k_end"], v_new)
            o = o * lax.rsqrt(jnp.mean(o * o, axis=-1, keepdims=True) + EPS) * gdn
            sl = pl.ds(g * dk, dk)
            r = rows[uu]
            o_ref[pl.ds(r, C), sl] = (o * za_ref[pl.ds(r, C), sl].astype(F32)).astype(o_ref.dtype)
        return tuple(states)

    lax.fori_loop(0, S // (C * unroll), body, tuple(jnp.zeros((dk, dk), F32) for _ in range(group)))


def _gdn(qkv3, wconv, col3, row3, za3, gdn_row, n_heads, group=2, unroll=4):
    B, S, W3 = qkv3.shape
    dk = W3 // (3 * n_heads)
    conv_k = wconv.shape[0]
    assert dk == GDN_CHUNK == LANES and n_heads % 8 == 0 and 8 % group == 0 and S % (GDN_CHUNK * unroll) == 0
    W = group * dk
    nb = n_heads // group
    seq = lambda off: pl.BlockSpec((None, S, W), lambda b, h: (b, 0, h + off))
    wsp = lambda off: pl.BlockSpec((conv_k, W), lambda b, h: (0, h + off))
    return pl.pallas_call(
        functools.partial(_gdn_kernel, n_heads=n_heads, conv_k=conv_k, group=group, unroll=unroll),
        grid=(B, nb),
        in_specs=[seq(0), seq(nb), seq(2 * nb),
                  wsp(0), wsp(nb), wsp(2 * nb),
                  pl.BlockSpec((None, S, LANES), lambda b, h: (b, 0, 0)),
                  pl.BlockSpec((None, 8, S), lambda b, h: (b, (n_heads + h * group) // 8, 0)),
                  pl.BlockSpec((None, S, W), lambda b, h: (b, 0, h)),
                  pl.BlockSpec((1, dk), lambda b, h: (0, 0))],
        out_specs=pl.BlockSpec((None, S, W), lambda b, h: (b, 0, h)),
        out_shape=jax.ShapeDtypeStruct((B, S, n_heads * dk), BF16),
        scratch_shapes=[pltpu.VMEM((S + 8, dk), F32)] + [pltpu.VMEM((S, W), F32)] * 3,
        compiler_params=_params("parallel", "arbitrary"),
        name="gated_deltanet",
    )(qkv3, qkv3, qkv3, wconv, wconv, wconv, col3, row3, za3, gdn_row)


def _conf_kernel(u_ref, halo_ref, zb_ref, w_ref, bdw_ref, lng_ref, lnb_ref, o_ref, win, cv, *, conv_k):
    i = pl.program_id(1)
    ts, Cn = cv.shape
    rs = 64
    off = HALO - (conv_k - 1)

    @pl.when(i == 0)
    def _():
        win[pl.ds(0, HALO), :] = jnp.zeros((HALO, Cn), F32)

    @pl.when(i > 0)
    def _():
        win[pl.ds(0, HALO), :] = halo_ref[...]

    win[pl.ds(HALO, ts), :] = u_ref[...]

    def col_block(cb, carry):
        c0 = pl.multiple_of(cb * LANES, LANES)
        w = w_ref[:, pl.ds(c0, LANES)]
        bias = bdw_ref[:, pl.ds(c0, LANES)]
        for r0 in range(0, ts, rs):
            acc = jnp.zeros((rs, LANES), F32)
            for k in range(conv_k):
                acc = acc + win[pl.ds(r0 + off + k, rs), pl.ds(c0, LANES)] * w[k:k + 1, :]
            cv[pl.ds(r0, rs), pl.ds(c0, LANES)] = acc + bias
        return carry

    lax.fori_loop(0, Cn // LANES, col_block, 0)

    y = cv[...]
    mu = jnp.mean(y, axis=-1, keepdims=True)
    d = y - mu
    var = jnp.mean(d * d, axis=-1, keepdims=True)
    y = d * lax.rsqrt(var + EPS) * lng_ref[...] + lnb_ref[...]
    o_ref[...] = (_silu(y) * zb_ref[...].astype(F32)).astype(o_ref.dtype)


def _conformer(u3, zb3, w_dw, bdw_row, lng_row, lnb_row, ts=256):
    B, S, Cn = u3.shape
    conv_k = w_dw.shape[0]
    ts = _tile(S, ts)
    hb = ts // HALO
    full = lambda shape: pl.BlockSpec(shape, lambda b, i: (0, 0))
    return pl.pallas_call(
        functools.partial(_conf_kernel, conv_k=conv_k),
        grid=(B, S // ts),
        in_specs=[pl.BlockSpec((None, ts, Cn), lambda b, i: (b, i, 0)),
                  pl.BlockSpec((None, HALO, Cn), lambda b, i: (b, jnp.maximum(i * hb - 1, 0), 0)),
                  pl.BlockSpec((None, ts, Cn), lambda b, i: (b, i, 0)),
                  full((conv_k, Cn)), full((1, Cn)), full((1, Cn)), full((1, Cn))],
        out_specs=pl.BlockSpec((None, ts, Cn), lambda b, i: (b, i, 0)),
        out_shape=jax.ShapeDtypeStruct((B, S, Cn), BF16),
        scratch_shapes=[pltpu.VMEM((HALO + ts, Cn), F32), pltpu.VMEM((ts, Cn), F32)],
        compiler_params=_params("parallel", "arbitrary"),
        name="conformer_conv",
    )(u3, u3, zb3, w_dw, bdw_row, lng_row, lnb_row)


def _merge_kernel(oa_ref, vb_ref, wa_ref, wb_ref, ga_ref, gb_ref, o_ref):
    ya = jnp.dot(oa_ref[...], wa_ref[...], preferred_element_type=F32)
    yb = jnp.dot(vb_ref[...], wb_ref[...], preferred_element_type=F32)
    mixed = ga_ref[...].astype(F32) * ya + gb_ref[...].astype(F32) * yb
    o_ref[...] = mixed.astype(o_ref.dtype)


def _merge(oa, vb, wa, wb, gates, tm=1024, tn=512):
    T, Da = oa.shape
    Db = vb.shape[1]
    N = wa.shape[1]
    tm, tn = _tile(T, tm), _tile(N, tn)
    nb = N // tn
    return pl.pallas_call(
        _merge_kernel,
        grid=(T // tm, nb),
        in_specs=[pl.BlockSpec((tm, Da), lambda i, j: (i, 0)),
                  pl.BlockSpec((tm, Db), lambda i, j: (i, 0)),
                  pl.BlockSpec((Da, tn), lambda i, j: (0, j)),
                  pl.BlockSpec((Db, tn), lambda i, j: (0, j)),
                  pl.BlockSpec((tm, tn), lambda i, j: (i, j)),
                  pl.BlockSpec((tm, tn), lambda i, j: (i, j + nb))],
        out_specs=pl.BlockSpec((tm, tn), lambda i, j: (i, j)),
        out_shape=jax.ShapeDtypeStruct((T, N), BF16),
        compiler_params=_params("parallel", "arbitrary"),
        name="branch_merge",
    )(oa, vb, wa, wb, gates, gates)


def _out_kernel(m_ref, w_ref, x_ref, g_ref, o_ref):
    y = jnp.dot(m_ref[...], w_ref[...], preferred_element_type=F32)
    y = y * lax.rsqrt(jnp.mean(y * y, axis=-1, keepdims=True) + EPS) * g_ref[...]
    o_ref[...] = x_ref[...] + y


def _out_proj(mixed, w, x2, g_row, tm=256):
    T, D = x2.shape
    tm = _tile(T, tm)
    return pl.pallas_call(
        _out_kernel,
        grid=(T // tm,),
        in_specs=[pl.BlockSpec((tm, D), lambda i: (i, 0)),
                  pl.BlockSpec((D, D), lambda i: (0, 0)),
                  pl.BlockSpec((tm, D), lambda i: (i, 0)),
                  pl.BlockSpec((1, D), lambda i: (0, 0))],
        out_specs=pl.BlockSpec((tm, D), lambda i: (i, 0)),
        out_shape=jax.ShapeDtypeStruct((T, D), F32),
        compiler_params=_params("parallel"),
        name="out_proj",
    )(mixed, w, x2, g_row)


def _ple_kernel(x_ref, p_ref, wg_ref, wp_ref, g_ref, o_ref):
    x1 = x_ref[...]
    gate = _sigmoid(jnp.dot(x1.astype(BF16), wg_ref[...], preferred_element_type=F32))
    e = jnp.dot(p_ref[...].astype(BF16), wp_ref[...], preferred_element_type=F32)
    y = gate * e
    y = y * lax.rsqrt(jnp.mean(y * y, axis=-1, keepdims=True) + EPS) * g_ref[...]
    o_ref[...] = x1 + y


def _ple(x1, p2, wg, wp, g_row, tm=256):
    T, D = x1.shape
    P = p2.shape[1]
    tm = _tile(T, tm)
    return pl.pallas_call(
        _ple_kernel,
        grid=(T // tm,),
        in_specs=[pl.BlockSpec((tm, D), lambda i: (i, 0)),
                  pl.BlockSpec((tm, P), lambda i: (i, 0)),
                  pl.BlockSpec((D, D), lambda i: (0, 0)),
                  pl.BlockSpec((P, D), lambda i: (0, 0)),
                  pl.BlockSpec((1, D), lambda i: (0, 0))],
        out_specs=pl.BlockSpec((tm, D), lambda i: (i, 0)),
        out_shape=jax.ShapeDtypeStruct((T, D), F32),
        compiler_params=_params("parallel"),
        name="ple",
    )(x1, p2, wg, wp, g_row)


def _layer(x, p, g_pre, w_in, b_gate, w_conv_qkv, a_log, dt_bias, g_dn_out, w_dw, b_dw,
           ln_g, ln_b, w_br_a, w_br_b, w_out, g_post, w_ple_gate, w_ple_proj, g_ple):
    B, S, D = x.shape
    T = B * S
    H = a_log.shape[0]
    dk = g_dn_out.shape[0]
    dn = H * dk
    cc = w_dw.shape[1]
    c0 = 3 * dn
    c1 = c0 + dn
    c2 = c1 + 2 * H
    c3 = c2 + 2 * cc
    c4 = c3 + cc
    row = lambda v: v.reshape(1, -1).astype(F32)
    lane_row = lambda v: jnp.zeros((1, LANES), F32).at[0, H:2 * H].set(v.astype(F32))

    x2 = x.reshape(T, D)
    h = _rmsnorm(x2, row(g_pre))

    wb = w_in.astype(BF16)
    w_ba = jnp.zeros((D, LANES), BF16).at[:, :2 * H].set(wb[:, c1:c2])
    qkv = _proj(h, wb[:, :c0], None, F32, name="proj_qkv")
    za = _proj(h, wb[:, c0:c1], "silu", BF16, name="proj_za")
    ba = _proj(h, w_ba, None, F32, name="proj_ba")
    u = _proj_glu(h, wb[:, c2:c3], F32)
    zb = _proj(h, wb[:, c3:c4], "silu", BF16, name="proj_zb")
    gates = _proj_gate(h, wb[:, c4:], row(b_gate), BF16)

    col, rowt = _gdn_prep(ba.reshape(B, S, LANES), lane_row(a_log), lane_row(dt_bias), H)
    oa = _gdn(qkv.reshape(B, S, c0), w_conv_qkv.astype(F32), col, rowt,
              za.reshape(B, S, dn), row(g_dn_out), H)
    vb = _conformer(u.reshape(B, S, cc), zb.reshape(B, S, cc), w_dw.astype(F32),
                    row(b_dw), row(ln_g), row(ln_b))

    mixed = _merge(oa.reshape(T, dn), vb.reshape(T, cc), w_br_a.astype(BF16),
                   w_br_b.astype(BF16), gates)
    x1 = _out_proj(mixed, w_out.astype(BF16), x2, row(g_post))
    out = _ple(x1, p.reshape(T, -1), w_ple_gate.astype(BF16), w_ple_proj.astype(BF16), row(g_ple))
    return out.reshape(B, S, D)


def kernel(x, p, g_pre, w_in, b_gate, w_conv_qkv, a_log, dt_bias, g_dn_out, w_dw, b_dw,
           ln_g, ln_b, w_br_a, w_br_b, w_out, g_post, w_ple_gate, w_ple_proj, g_ple):
    for i in range(p.shape[0]):
        x = _layer(x, p[i], g_pre[i], w_in[i], b_gate[i], w_conv_qkv[i], a_log[i], dt_bias[i],
                   g_dn_out[i], w_dw[i], b_dw[i], ln_g[i], ln_b[i], w_br_a[i], w_br_b[i],
                   w_out[i], g_post[i], w_ple_gate[i], w_ple_proj[i], g_ple[i])
    return x
```

```python
import functools

import jax
import jax.numpy as jnp
from jax import lax
from jax.experimental import pallas as pl
from jax.experimental.pallas import tpu as pltpu

EPS = 1e-6
GDN_CHUNK = 128
LANES = 128
HALO = 32
VMEM_LIMIT = 56 * 1024 * 1024

F32 = jnp.float32
BF16 = jnp.bfloat16
HIGHEST = lax.Precision.HIGHEST


def _params(*sem):
    return pltpu.CompilerParams(dimension_semantics=sem, vmem_limit_bytes=VMEM_LIMIT)


def _tile(n, t):
    t = min(t, n)
    while n % t:
        t //= 2
    return t


def _sigmoid(x):
    return 0.5 * jnp.tanh(0.5 * x) + 0.5


def _silu(x):
    hx = 0.5 * x
    return hx + hx * jnp.tanh(hx)


def _dot(a, b):
    return jnp.dot(a.astype(BF16), b.astype(BF16), preferred_element_type=F32)


def _dot_nt(a, b):
    return lax.dot_general(a.astype(BF16), b.astype(BF16), (((1,), (1,)), ((), ())),
                           preferred_element_type=F32)


def _dot_tn(a, b):
    return lax.dot_general(a.astype(BF16), b.astype(BF16), (((0,), (0,)), ((), ())),
                           preferred_element_type=F32)


def _dot_f32(a, b):
    return jnp.dot(a, b, preferred_element_type=F32, precision=HIGHEST)


def _rmsnorm_kernel(x_ref, g_ref, o_ref):
    x = x_ref[...]
    ms = jnp.mean(x * x, axis=-1, keepdims=True)
    o_ref[...] = (x * lax.rsqrt(ms + EPS) * g_ref[...]).astype(o_ref.dtype)


def _rmsnorm(x2, g_row, tr=512):
    T, D = x2.shape
    tr = _tile(T, tr)
    return pl.pallas_call(
        _rmsnorm_kernel,
        grid=(T // tr,),
        in_specs=[pl.BlockSpec((tr, D), lambda i: (i, 0)),
                  pl.BlockSpec((1, D), lambda i: (0, 0))],
        out_specs=pl.BlockSpec((tr, D), lambda i: (i, 0)),
        out_shape=jax.ShapeDtypeStruct((T, D), BF16),
        compiler_params=_params("parallel"),
        name="pre_rmsnorm",
    )(x2, g_row)


def _proj_kernel(h_ref, w_ref, o_ref, *, act):
    acc = jnp.dot(h_ref[...], w_ref[...], preferred_element_type=F32)
    if act == "silu":
        acc = _silu(acc)
    o_ref[...] = acc.astype(o_ref.dtype)


def _proj(h, w, col0, ncols, act, out_dtype, tm=1024, tn=1024, name="proj"):
    T, D = h.shape
    tm, tn = _tile(T, tm), _tile(ncols, tn)
    assert col0 % tn == 0
    jb = col0 // tn
    return pl.pallas_call(
        functools.partial(_proj_kernel, act=act),
        grid=(T // tm, ncols // tn),
        in_specs=[pl.BlockSpec((tm, D), lambda i, j: (i, 0)),
                  pl.BlockSpec((D, tn), lambda i, j: (0, j + jb))],
        out_specs=pl.BlockSpec((tm, tn), lambda i, j: (i, j)),
        out_shape=jax.ShapeDtypeStruct((T, ncols), out_dtype),
        compiler_params=_params("parallel", "arbitrary"),
        name=name,
    )(h, w)


def _glu_kernel(h_ref, wa_ref, wb_ref, o_ref):
    h = h_ref[...]
    a = jnp.dot(h, wa_ref[...], preferred_element_type=F32)
    b = jnp.dot(h, wb_ref[...], preferred_element_type=F32)
    o_ref[...] = (a * _sigmoid(b)).astype(o_ref.dtype)


def _proj_glu(h, w, col0, ncols, out_dtype, tm=1024, tn=512):
    T, D = h.shape
    tm, tn = _tile(T, tm), _tile(ncols, tn)
    assert col0 % tn == 0
    ja = col0 // tn
    jb = ja + ncols // tn
    return pl.pallas_call(
        _glu_kernel,
        grid=(T // tm, ncols // tn),
        in_specs=[pl.BlockSpec((tm, D), lambda i, j: (i, 0)),
                  pl.BlockSpec((D, tn), lambda i, j: (0, j + ja)),
                  pl.BlockSpec((D, tn), lambda i, j: (0, j + jb))],
        out_specs=pl.BlockSpec((tm, tn), lambda i, j: (i, j)),
        out_shape=jax.ShapeDtypeStruct((T, ncols), out_dtype),
        compiler_params=_params("parallel", "arbitrary"),
        name="proj_glu",
    )(h, w, w)


def _gate_kernel(h_ref, w_ref, b_ref, o_ref):
    acc = jnp.dot(h_ref[...], w_ref[...], preferred_element_type=F32)
    o_ref[...] = _sigmoid(acc + b_ref[...]).astype(o_ref.dtype)


def _proj_gate(h, w, col0, ncols, b_row, out_dtype, tm=1024, tn=1024):
    T, D = h.shape
    tm, tn = _tile(T, tm), _tile(ncols // 2, tn)
    assert col0 % tn == 0
    jb = col0 // tn
    return pl.pallas_call(
        _gate_kernel,
        grid=(T // tm, ncols // tn),
        in_specs=[pl.BlockSpec((tm, D), lambda i, j: (i, 0)),
                  pl.BlockSpec((D, tn), lambda i, j: (0, j + jb)),
                  pl.BlockSpec((1, tn), lambda i, j: (0, j))],
        out_specs=pl.BlockSpec((tm, tn), lambda i, j: (i, j)),
        out_shape=jax.ShapeDtypeStruct((T, ncols), out_dtype),
        compiler_params=_params("parallel", "arbitrary"),
        name="proj_gate",
    )(h, w, b_row)


def _gdn_prep_kernel(ba_ref, alog_ref, dtb_ref, col_ref, row_ref, *, n_heads):
    S = ba_ref.shape[0]
    C = GDN_CHUNK
    lane = lax.broadcasted_iota(jnp.int32, (C, LANES), 1)
    ri = lax.broadcasted_iota(jnp.int32, (C, C), 0)
    ci = lax.broadcasted_iota(jnp.int32, (C, C), 1)
    tril = (ri >= ci).astype(F32)
    is_decay = (lane >= n_heads) & (lane < 2 * n_heads)
    for c in range(S // C):
        raw = ba_ref[pl.ds(c * C, C), :]
        z = raw + dtb_ref[...]
        softplus = jnp.maximum(z, 0.0) + jnp.log(1.0 + jnp.exp(-jnp.abs(z)))
        g = jnp.where(is_decay, -jnp.exp(alog_ref[...]) * softplus, 0.0)
        gc = _dot_f32(tril, g)
        both = jnp.where(lane < n_heads, _sigmoid(raw), gc)
        col_ref[pl.ds(c * C, C), :] = both
        row_ref[:, pl.ds(c * C, C)] = both.T


def _gdn_prep(ba3, alog_row, dtb_row, n_heads):
    B, S, _ = ba3.shape
    row = pl.BlockSpec((1, LANES), lambda b: (0, 0))
    return pl.pallas_call(
        functools.partial(_gdn_prep_kernel, n_heads=n_heads),
        grid=(B,),
        in_specs=[pl.BlockSpec((None, S, LANES), lambda b: (b, 0, 0)), row, row],
        out_specs=[pl.BlockSpec((None, S, LANES), lambda b: (b, 0, 0)),
                   pl.BlockSpec((None, LANES, S), lambda b: (b, 0, 0))],
        out_shape=[jax.ShapeDtypeStruct((B, S, LANES), F32),
                   jax.ShapeDtypeStruct((B, LANES, S), F32)],
        compiler_params=_params("parallel"),
        name="gdn_prep",
    )(ba3, alog_row, dtb_row)


def _gdn_kernel(q_ref, k_ref, v_ref, wq_ref, wk_ref, wv_ref, col_ref, rowg_ref,
                za_ref, gdn_ref, o_ref, xp, qs, ks, vs, *, n_heads, conv_k, group, unroll):
    S, W = qs.shape
    dk = W // group
    C = GDN_CHUNK
    h0 = pl.program_id(1) * group
    pad = 8
    rb = min(128, S)

    def conv_silu(x_ref, w_ref, dst, l2):
        xp[pl.ds(0, pad), :] = jnp.zeros((pad, dk), F32)
        for g in range(group):
            sl = pl.ds(g * dk, dk)
            xp[pl.ds(pad, S), :] = x_ref[:, sl]
            w = w_ref[:, sl]
            for r0 in range(0, S, rb):
                acc = jnp.zeros((rb, dk), F32)
                for j in range(conv_k):
                    sh = conv_k - 1 - j
                    acc = acc + xp[pl.ds(pad + r0 - sh, rb), :] * w[j:j + 1, :]
                y = _silu(acc)
                if l2:
                    y = y * lax.rsqrt(jnp.sum(y * y, axis=-1, keepdims=True) + EPS)
                dst[pl.ds(r0, rb), sl] = y

    conv_silu(q_ref, wq_ref, qs, True)
    conv_silu(k_ref, wk_ref, ks, True)
    conv_silu(v_ref, wv_ref, vs, False)

    lane = lax.broadcasted_iota(jnp.int32, (C, LANES), 1)
    sub = lax.broadcasted_iota(jnp.int32, (8, C), 0)
    ri = lax.broadcasted_iota(jnp.int32, (C, C), 0)
    ci = lax.broadcasted_iota(jnp.int32, (C, C), 1)
    causal = ri >= ci
    strict = ri > ci
    scale = dk ** -0.5
    gdn = gdn_ref[...]

    def lane_column(blk, idx):
        col = jnp.sum(jnp.where(lane == idx, blk, 0.0), axis=-1, keepdims=True)
        return jnp.broadcast_to(col, (C, dk))

    def load_chunk(g, r):
        sl = pl.ds(g * dk, dk)
        qc = qs[pl.ds(r, C), sl] * scale
        kc = ks[pl.ds(r, C), sl]
        vc = vs[pl.ds(r, C), sl]
        blk = col_ref[pl.ds(r, C), :]
        b_col = lane_column(blk, h0 + g)
        g_col = lane_column(blk, h0 + g + n_heads)
        rows = rowg_ref[:, pl.ds(r, C)]
        g_row = jnp.sum(jnp.where(sub == (h0 + g) % 8, rows, 0.0), axis=0, keepdims=True)
        decay = jnp.where(causal, jnp.exp(jnp.where(causal, g_col - g_row, 0.0)), 0.0)
        kb = kc * b_col
        eg = jnp.exp(g_col)
        g_last = g_col[C - 1:C, :]
        return dict(qc=qc, kc=kc, kb=kb, decay=decay, qd=qc * eg,
                    rhs=jnp.concatenate([vc * b_col, kb * eg], axis=1),
                    k_end=kc * jnp.exp(g_last - g_col), e_last=jnp.exp(g_last))

    def body(i, states):
        states = list(states)
        chains = [(uu, g) for uu in range(unroll) for g in range(group)]
        rows = [pl.multiple_of((i * unroll + uu) * C, C) for uu in range(unroll)]
        ops = [load_chunk(g, rows[uu]) for uu, g in chains]
        kqs = [_dot_nt(jnp.concatenate([c["kb"], c["qc"]], axis=0), c["kc"]) for c in ops]
        qks = [kq[C:] * c["decay"] for kq, c in zip(kqs, ops)]
        ms = [-(kq[:C] * jnp.where(strict, c["decay"], 0.0)) for kq, c in zip(kqs, ops)]
        ts = list(ms)
        p = 2
        while p < C:
            ms = [_dot(m, m) for m in ms]
            ts = [t + m + _dot(t, m) for t, m in zip(ts, ms)]
            p *= 2
        uws = [c["rhs"] + _dot(t, c["rhs"]) for t, c in zip(ts, ops)]
        for (uu, g), c, uw, qk in zip(chains, ops, uws, qks):
            st = states[g]
            ws = _dot(jnp.concatenate([uw[:, dk:], c["qd"]], axis=0), st)
            v_new = uw[:, :dk] - ws[:C]
            o = ws[C:] + _dot(qk, v_new)
            states[g] = st * c["e_last"] + _dot_tn(c["k_end"], v_new)
            o = o * lax.rsqrt(jnp.mean(o * o, axis=-1, keepdims=True) + EPS) * gdn
            sl = pl.ds(g * dk, dk)
            r = rows[uu]
            o_ref[pl.ds(r, C), sl] = (o * za_ref[pl.ds(r, C), sl].astype(F32)).astype(o_ref.dtype)
        return tuple(states)

    lax.fori_loop(0, S // (C * unroll), body, tuple(jnp.zeros((dk, dk), F32) for _ in range(group)))


def _gdn(qkv3, wconv, col3, row3, za3, gdn_row, n_heads, group=2, unroll=8):
    B, S, W3 = qkv3.shape
    dk = W3 // (3 * n_heads)
    conv_k = wconv.shape[0]
    assert dk == GDN_CHUNK == LANES and n_heads % 8 == 0 and 8 % group == 0 and S % (GDN_CHUNK * unroll) == 0
    W = group * dk
    nb = n_heads // group
    seq = lambda off: pl.BlockSpec((None, S, W), lambda b, h: (b, 0, h + off))
    wsp = lambda off: pl.BlockSpec((conv_k, W), lambda b, h: (0, h + off))
    return pl.pallas_call(
        functools.partial(_gdn_kernel, n_heads=n_heads, conv_k=conv_k, group=group, unroll=unroll),
        grid=(B, nb),
        in_specs=[seq(0), seq(nb), seq(2 * nb),
                  wsp(0), wsp(nb), wsp(2 * nb),
                  pl.BlockSpec((None, S, LANES), lambda b, h: (b, 0, 0)),
                  pl.BlockSpec((None, 8, S), lambda b, h: (b, (n_heads + h * group) // 8, 0)),
                  pl.BlockSpec((None, S, W), lambda b, h: (b, 0, h)),
                  pl.BlockSpec((1, dk), lambda b, h: (0, 0))],
        out_specs=pl.BlockSpec((None, S, W), lambda b, h: (b, 0, h)),
        out_shape=jax.ShapeDtypeStruct((B, S, n_heads * dk), BF16),
        scratch_shapes=[pltpu.VMEM((S + 8, dk), F32)] + [pltpu.VMEM((S, W), F32)] * 3,
        compiler_params=_params("parallel", "arbitrary"),
        name="gated_deltanet",
    )(qkv3, qkv3, qkv3, wconv, wconv, wconv, col3, row3, za3, gdn_row)


def _conf_kernel(u_ref, halo_ref, zb_ref, w_ref, bdw_ref, lng_ref, lnb_ref, o_ref, win, cv, *, conv_k):
    i = pl.program_id(1)
    ts, Cn = cv.shape
    rs = min(128, ts)
    lr = 16
    ln_par = 4
    off = HALO - (conv_k - 1)

    def col_block(cb, carry):
        cs = pl.ds(pl.multiple_of(cb * LANES, LANES), LANES)
        win[pl.ds(0, HALO), :] = jnp.where(i > 0, halo_ref[:, cs], 0.0)
        win[pl.ds(HALO, ts), :] = u_ref[:, cs]
        w = w_ref[:, cs]
        bias = bdw_ref[:, cs]
        for r0 in range(0, ts, rs):
            acc = jnp.broadcast_to(bias, (rs, LANES))
            for k in range(conv_k):
                acc = acc + win[pl.ds(r0 + off + k, rs), :] * w[k:k + 1, :]
            cv[pl.ds(r0, rs), cs] = acc
        return carry

    lax.fori_loop(0, Cn // LANES, col_block, 0)

    def ln_blocks(it, carry):
        rows = [pl.ds(pl.multiple_of((it * ln_par + q) * lr, lr), lr) for q in range(ln_par)]
        ys = [cv[r, :] for r in rows]
        mus = [jnp.mean(y, axis=-1, keepdims=True) for y in ys]
        ds = [y - mu for y, mu in zip(ys, mus)]
        vs = [jnp.mean(d * d, axis=-1, keepdims=True) for d in ds]
        for r, d, var in zip(rows, ds, vs):
            yn = d * lax.rsqrt(var + EPS) * lng_ref[...] + lnb_ref[...]
            o_ref[r, :] = (_silu(yn) * zb_ref[r, :].astype(F32)).astype(o_ref.dtype)
        return carry

    lax.fori_loop(0, ts // (lr * ln_par), ln_blocks, 0)


def _conformer(u3, zb3, w_dw, bdw_row, lng_row, lnb_row, ts=512):
    B, S, Cn = u3.shape
    conv_k = w_dw.shape[0]
    ts = _tile(S, ts)
    hb = ts // HALO
    assert conv_k - 1 <= HALO and ts % HALO == 0 and ts % 64 == 0
    full = lambda shape: pl.BlockSpec(shape, lambda b, i: (0, 0))
    return pl.pallas_call(
        functools.partial(_conf_kernel, conv_k=conv_k),
        grid=(B, S // ts),
        in_specs=[pl.BlockSpec((None, ts, Cn), lambda b, i: (b, i, 0)),
                  pl.BlockSpec((None, HALO, Cn), lambda b, i: (b, jnp.maximum(i * hb - 1, 0), 0)),
                  pl.BlockSpec((None, ts, Cn), lambda b, i: (b, i, 0)),
                  full((conv_k, Cn)), full((1, Cn)), full((1, Cn)), full((1, Cn))],
        out_specs=pl.BlockSpec((None, ts, Cn), lambda b, i: (b, i, 0)),
        out_shape=jax.ShapeDtypeStruct((B, S, Cn), BF16),
        scratch_shapes=[pltpu.VMEM((HALO + ts, LANES), F32), pltpu.VMEM((ts, Cn), F32)],
        compiler_params=_params("parallel", "arbitrary"),
        name="conformer_conv",
    )(u3, u3, zb3, w_dw, bdw_row, lng_row, lnb_row)


def _merge_kernel(oa_ref, vb_ref, wa_ref, wb_ref, ga_ref, gb_ref, o_ref):
    ya = jnp.dot(oa_ref[...], wa_ref[...], preferred_element_type=F32)
    yb = jnp.dot(vb_ref[...], wb_ref[...], preferred_element_type=F32)
    mixed = ga_ref[...].astype(F32) * ya + gb_ref[...].astype(F32) * yb
    o_ref[...] = mixed.astype(o_ref.dtype)


def _merge(oa, vb, wa, wb, gates, tm=1024, tn=512):
    T, Da = oa.shape
    Db = vb.shape[1]
    N = wa.shape[1]
    tm, tn = _tile(T, tm), _tile(N, tn)
    nb = N // tn
    return pl.pallas_call(
        _merge_kernel,
        grid=(T // tm, nb),
        in_specs=[pl.BlockSpec((tm, Da), lambda i, j: (i, 0)),
                  pl.BlockSpec((tm, Db), lambda i, j: (i, 0)),
                  pl.BlockSpec((Da, tn), lambda i, j: (0, j)),
                  pl.BlockSpec((Db, tn), lambda i, j: (0, j)),
                  pl.BlockSpec((tm, tn), lambda i, j: (i, j)),
                  pl.BlockSpec((tm, tn), lambda i, j: (i, j + nb))],
        out_specs=pl.BlockSpec((tm, tn), lambda i, j: (i, j)),
        out_shape=jax.ShapeDtypeStruct((T, N), BF16),
        compiler_params=_params("parallel", "arbitrary"),
        name="branch_merge",
    )(oa, vb, wa, wb, gates, gates)


def _out_kernel(m_ref, w_ref, x_ref, g_ref, o_ref):
    y = jnp.dot(m_ref[...], w_ref[...], preferred_element_type=F32)
    y = y * lax.rsqrt(jnp.mean(y * y, axis=-1, keepdims=True) + EPS) * g_ref[...]
    o_ref[...] = x_ref[...] + y


def _out_proj(mixed, w, x2, g_row, tm=512):
    T, D = x2.shape
    tm = _tile(T, tm)
    return pl.pallas_call(
        _out_kernel,
        grid=(T // tm,),
        in_specs=[pl.BlockSpec((tm, D), lambda i: (i, 0)),
                  pl.BlockSpec((D, D), lambda i: (0, 0), pipeline_mode=pl.Buffered(1)),
                  pl.BlockSpec((tm, D), lambda i: (i, 0)),
                  pl.BlockSpec((1, D), lambda i: (0, 0))],
        out_specs=pl.BlockSpec((tm, D), lambda i: (i, 0)),
        out_shape=jax.ShapeDtypeStruct((T, D), F32),
        compiler_params=_params("parallel"),
        name="out_proj",
    )(mixed, w, x2, g_row)


def _ple_kernel(x_ref, p_ref, wg_ref, wp_ref, g_ref, o_ref):
    x1 = x_ref[...]
    gate = _sigmoid(jnp.dot(x1.astype(BF16), wg_ref[...], preferred_element_type=F32))
    e = jnp.dot(p_ref[...].astype(BF16), wp_ref[...], preferred_element_type=F32)
    y = gate * e
    y = y * lax.rsqrt(jnp.mean(y * y, axis=-1, keepdims=True) + EPS) * g_ref[...]
    o_ref[...] = x1 + y


def _ple(x1, p2, wg, wp, g_row, tm=512):
    T, D = x1.shape
    P = p2.shape[1]
    tm = _tile(T, tm)
    return pl.pallas_call(
        _ple_kernel,
        grid=(T // tm,),
        in_specs=[pl.BlockSpec((tm, D), lambda i: (i, 0)),
                  pl.BlockSpec((tm, P), lambda i: (i, 0)),
                  pl.BlockSpec((D, D), lambda i: (0, 0), pipeline_mode=pl.Buffered(1)),
                  pl.BlockSpec((P, D), lambda i: (0, 0), pipeline_mode=pl.Buffered(1)),
                  pl.BlockSpec((1, D), lambda i: (0, 0))],
        out_specs=pl.BlockSpec((tm, D), lambda i: (i, 0)),
        out_shape=jax.ShapeDtypeStruct((T, D), F32),
        compiler_params=_params("parallel"),
        name="ple",
    )(x1, p2, wg, wp, g_row)


def _layer(x, p, g_pre, w_in, b_gate, w_conv_qkv, a_log, dt_bias, g_dn_out, w_dw, b_dw,
           ln_g, ln_b, w_br_a, w_br_b, w_out, g_post, w_ple_gate, w_ple_proj, g_ple):
    B, S, D = x.shape
    T = B * S
    H = a_log.shape[0]
    dk = g_dn_out.shape[0]
    dn = H * dk
    cc = w_dw.shape[1]
    c0 = 3 * dn
    c1 = c0 + dn
    c2 = c1 + 2 * H
    c3 = c2 + 2 * cc
    c4 = c3 + cc
    row = lambda v: v.reshape(1, -1).astype(F32)
    lane_row = lambda v: jnp.zeros((1, LANES), F32).at[0, H:2 * H].set(v.astype(F32))

    x2 = x.reshape(T, D)
    h = _rmsnorm(x2, row(g_pre))

    w_a = w_in[:, :c1].astype(BF16)
    w_b = jnp.concatenate([w_in[:, c2:], w_in[:, c1:c2],
                           jnp.zeros((D, LANES - 2 * H), w_in.dtype)], axis=1).astype(BF16)
    qkv = _proj(h, w_a, 0, c0, None, F32, name="proj_qkv")
    za = _proj(h, w_a, c0, dn, "silu", BF16, name="proj_za")
    u = _proj_glu(h, w_b, 0, cc, F32)
    zb = _proj(h, w_b, 2 * cc, cc, "silu", BF16, name="proj_zb")
    gates = _proj_gate(h, w_b, 3 * cc, 2 * D, row(b_gate), BF16)
    ba = _proj(h, w_b, 3 * cc + 2 * D, LANES, None, F32, name="proj_ba")

    col, rowt = _gdn_prep(ba.reshape(B, S, LANES), lane_row(a_log), lane_row(dt_bias), H)
    oa = _gdn(qkv.reshape(B, S, c0), w_conv_qkv.astype(F32), col, rowt,
              za.reshape(B, S, dn), row(g_dn_out), H)
    vb = _conformer(u.reshape(B, S, cc), zb.reshape(B, S, cc), w_dw.astype(F32),
                    row(b_dw), row(ln_g), row(ln_b))

    mixed = _merge(oa.reshape(T, dn), vb.reshape(T, cc), w_br_a.astype(BF16),
                   w_br_b.astype(BF16), gates)
    x1 = _out_proj(mixed, w_out.astype(BF16), x2, row(g_post))
    out = _ple(x1, p.reshape(T, -1), w_ple_gate.astype(BF16), w_ple_proj.astype(BF16), row(g_ple))
    return out.reshape(B, S, D)


def kernel(x, p, g_pre, w_in, b_gate, w_conv_qkv, a_log, dt_bias, g_dn_out, w_dw, b_dw,
           ln_g, ln_b, w_br_a, w_br_b, w_out, g_post, w_ple_gate, w_ple_proj, g_ple):
    for i in range(p.shape[0]):
        x = _layer(x, p[i], g_pre[i], w_in[i], b_gate[i], w_conv_qkv[i], a_log[i], dt_bias[i],
                   g_dn_out[i], w_dw[i], b_dw[i], ln_g[i], ln_b[i], w_br_a[i], w_br_b[i],
                   w_out[i], g_post[i], w_ple_gate[i], w_ple_proj[i], g_ple[i])
    return x
```
